```python
import math
import jax, jax.numpy as jnp
from jax import lax
import numpy as np

D_MODEL = 1024
BATCH = 32
SEQ = 2048
DEPTH = 1

MIX_WIDTH = D_MODEL
POOL_WIDTH = MIX_WIDTH // 2
POOL_WINDOWS = (2, 4, 8, 16)
POOL_GROUPS = len(POOL_WINDOWS)
POOL_CH = POOL_WIDTH // POOL_GROUPS
SGU_WIDTH = MIX_WIDTH - POOL_WIDTH
SGU_HEADS = 4
SGU_HD = SGU_WIDTH // SGU_HEADS
CHUNK = 128
IN_COLS = POOL_WIDTH + 2 * SGU_WIDTH
D_FF = int(math.ceil((8 * D_MODEL / 3) / 256) * 256)
LN_EPS = 1e-5
DEEPNORM_ALPHA = float((2.0 * DEPTH) ** 0.25)
DEEPNORM_BETA = float((8.0 * DEPTH) ** -0.25)

kernel_name = "hybrid_pool_sgu_deepnorm_layer"


def layer_norm(x, g, b):
    xf = x.astype(jnp.float32)
    mu = jnp.mean(xf, axis=-1, keepdims=True)
    var = jnp.mean(jnp.square(xf - mu), axis=-1, keepdims=True)
    out = (xf - mu) * lax.rsqrt(var + LN_EPS)
    return (out * g.astype(jnp.float32) + b.astype(jnp.float32)).astype(x.dtype)


def causal_multiscale_pool(xp):
    S = xp.shape[1]
    xf = xp.astype(jnp.float32)
    cs0 = jnp.pad(jnp.cumsum(xf, axis=1), ((0, 0), (1, 0), (0, 0)))
    pos = jnp.arange(1, S + 1, dtype=jnp.int32)
    outs = []
    for g, w in enumerate(POOL_WINDOWS):
        sl = slice(g * POOL_CH, (g + 1) * POOL_CH)
        c = cs0[..., sl]
        lower = jnp.pad(c, ((0, 0), (w - 1, 0), (0, 0)))[:, :S]
        cnt = jnp.minimum(pos, w).astype(jnp.float32)[None, :, None]
        outs.append((c[:, 1:] - lower) / cnt - xf[..., sl])
    return jnp.stack(outs, axis=2)


def spatial_gating(z, ln_g, ln_b, w_s, b_s):
    B, S, _ = z.shape
    u, v = z[..., :SGU_WIDTH], z[..., SGU_WIDTH:]
    v = layer_norm(v, ln_g, ln_b)
    v = v.reshape(B, S // CHUNK, CHUNK, SGU_HEADS, SGU_HD)
    mask = jnp.tril(jnp.ones((CHUNK, CHUNK), dtype=w_s.dtype))
    ws = w_s * mask[None]
    mixed = jnp.einsum('hts,bnshd->bnthd', ws, v)
    mixed = mixed + jnp.transpose(b_s)[None, None, :, :, None]
    return u * mixed.reshape(B, S, SGU_WIDTH)


def swiglu_ffn(h, w_gate_up, w_down):
    gu = jnp.einsum('bsd,df->bsf', h, w_gate_up)
    gate, up = gu[..., :D_FF], gu[..., D_FF:]
    return jnp.einsum('bsf,fd->bsd', jax.nn.silu(gate) * up, w_down)


def setup_inputs(seed: int = 0) -> dict:
    key = jax.random.key(seed)
    ks = jax.random.split(key, 16)
    f32 = jnp.float32
    def nrm(k, shape, scale):
        return jax.random.normal(k, shape, f32) * scale
    return {
        "x": jax.random.normal(ks[0], (BATCH, SEQ, D_MODEL), f32),
        "w_in": nrm(ks[1], (DEPTH, D_MODEL, IN_COLS), D_MODEL ** -0.5),
        "pool_w": nrm(ks[2], (DEPTH, POOL_GROUPS, POOL_CH, POOL_CH), POOL_CH ** -0.5),
        "pool_scale": 1.0 + nrm(ks[3], (DEPTH, POOL_WIDTH), 0.1),
        "sgu_ln_g": 1.0 + nrm(ks[4], (DEPTH, SGU_WIDTH), 0.01),
        "sgu_ln_b": nrm(ks[5], (DEPTH, SGU_WIDTH), 0.01),
        "sgu_w": nrm(ks[6], (DEPTH, SGU_HEADS, CHUNK, CHUNK), CHUNK ** -0.5),
        "sgu_b": 1.0 + nrm(ks[7], (DEPTH, SGU_HEADS, CHUNK), 0.01),
        "w_out": nrm(ks[8], (DEPTH, MIX_WIDTH, D_MODEL), MIX_WIDTH ** -0.5 * DEEPNORM_BETA),
        "ln1_g": 1.0 + nrm(ks[9], (DEPTH, D_MODEL), 0.01),
        "ln1_b": nrm(ks[10], (DEPTH, D_MODEL), 0.01),
        "w_gate_up": nrm(ks[11], (DEPTH, D_MODEL, 2 * D_FF), D_MODEL ** -0.5),
        "w_down": nrm(ks[12], (DEPTH, D_FF, D_MODEL), D_FF ** -0.5 * DEEPNORM_BETA),
        "ln2_g": 1.0 + nrm(ks[13], (DEPTH, D_MODEL), 0.01),
        "ln2_b": nrm(ks[14], (DEPTH, D_MODEL), 0.01),
    }


def reference(x, w_in, pool_w, pool_scale, sgu_ln_g, sgu_ln_b, sgu_w, sgu_b,
              w_out, ln1_g, ln1_b, w_gate_up, w_down, ln2_g, ln2_b):
    B, S, _ = x.shape
    alpha = jnp.asarray(DEEPNORM_ALPHA, dtype=x.dtype)
    for l in range(DEPTH):
        proj = jnp.einsum('bsd,dc->bsc', x, w_in[l])
        xp = proj[..., :POOL_WIDTH]
        zg = jax.nn.gelu(proj[..., POOL_WIDTH:], approximate=False)
        pooled = causal_multiscale_pool(xp)
        pool_out = jnp.einsum('bsgc,gcd->bsgd', pooled, pool_w[l].astype(jnp.float32))
        pool_out = (pool_out.reshape(B, S, POOL_WIDTH) * pool_scale[l]).astype(x.dtype)
        sgu_out = spatial_gating(zg, sgu_ln_g[l], sgu_ln_b[l], sgu_w[l], sgu_b[l])
        mix = jnp.concatenate([pool_out, sgu_out], axis=-1)
        mix = jnp.einsum('bsc,cd->bsd', mix, w_out[l])
        h = layer_norm(alpha * x + mix, ln1_g[l], ln1_b[l])
        x = layer_norm(alpha * h + swiglu_ffn(h, w_gate_up[l], w_down[l]), ln2_g[l], ln2_b[l])
    return x
```

```python
import functools
import math

import jax
import jax.numpy as jnp
from jax import lax
from jax.experimental import pallas as pl
from jax.experimental.pallas import tpu as pltpu

D_MODEL = 1024
POOL_WIDTH = 512
POOL_WINDOWS = (2, 4, 8, 16)
POOL_CH = 128
SGU_WIDTH = 512
SGU_HEADS = 4
SGU_HD = 128
CHUNK = 128
IN_COLS = POOL_WIDTH + 2 * SGU_WIDTH
D_FF = 2816
LN_EPS = 1e-5
DEPTH = 1
DEEPNORM_ALPHA = float((2.0 * DEPTH) ** 0.25)

HALO = 16
TM = 256
VMEM_LIMIT_BYTES = 56 * 1024 * 1024


def _layer_norm(v, g, b):
    mu = jnp.mean(v, axis=-1, keepdims=True)
    d = v - mu
    var = jnp.mean(d * d, axis=-1, keepdims=True)
    return d * lax.rsqrt(var + LN_EPS) * g + b


def _gelu_exact(z):
    return 0.5 * z * (1.0 + lax.erf(z * math.sqrt(0.5)))


def _dot(a, b):
    return jnp.dot(a, b, preferred_element_type=jnp.float32)


def _layer_kernel(x_ref, w_in_ref, pool_w_ref, pool_scale_ref, sgu_g_ref, sgu_b_ref,
                  sgu_w_ref, sgu_bias_ref, w_out_ref, ln1_g_ref, ln1_b_ref,
                  w_gu_ref, w_down_ref, ln2_g_ref, ln2_b_ref, o_ref, carry_ref,
                  *, tiles_per_seq):
    tile_in_seq = pl.program_id(0) % tiles_per_seq

    @pl.when(tile_in_seq == 0)
    def _():
        carry_ref[...] = jnp.zeros_like(carry_ref)

    x = x_ref[...]
    proj = _dot(x.astype(jnp.bfloat16), w_in_ref[...])
    xp = proj[:, :POOL_WIDTH]
    zg = _gelu_exact(proj[:, POOL_WIDTH:])

    ext = jnp.concatenate([carry_ref[...], xp], axis=0)
    carry_ref[...] = xp[TM - HALO:, :]
    pos = tile_in_seq * TM + lax.broadcasted_iota(jnp.int32, (TM, POOL_CH), 0) + 1
    pooled = []
    for g, w in enumerate(POOL_WINDOWS):
        s = ext[:, g * POOL_CH:(g + 1) * POOL_CH]
        shift = 1
        while shift < w:
            s = s + pltpu.roll(s, shift, 0)
            shift *= 2
        cnt = jnp.minimum(pos, w).astype(jnp.float32)
        pooled.append(s[HALO:, :] / cnt - xp[:, g * POOL_CH:(g + 1) * POOL_CH])
    pool_out = []
    for g in range(len(POOL_WINDOWS)):
        pool_out.append(_dot(pooled[g].astype(jnp.bfloat16), pool_w_ref[g]))
    pool_out = jnp.concatenate(pool_out, axis=-1) * pool_scale_ref[...]

    u = zg[:, :SGU_WIDTH]
    v = _layer_norm(zg[:, SGU_WIDTH:], sgu_g_ref[...], sgu_b_ref[...]).astype(jnp.bfloat16)
    tril = (lax.broadcasted_iota(jnp.int32, (CHUNK, CHUNK), 0)
            >= lax.broadcasted_iota(jnp.int32, (CHUNK, CHUNK), 1))
    rows = []
    for c in range(TM // CHUNK):
        heads = []
        for h in range(SGU_HEADS):
            ws = jnp.where(tril, sgu_w_ref[h], 0.0).astype(jnp.bfloat16)
            vch = v[c * CHUNK:(c + 1) * CHUNK, h * SGU_HD:(h + 1) * SGU_HD]
            bias = jnp.broadcast_to(sgu_bias_ref[:, h:h + 1], (CHUNK, SGU_HD))
            heads.append(_dot(ws, vch) + bias)
        rows.append(jnp.concatenate(heads, axis=-1))
    sgu_out = u * jnp.concatenate(rows, axis=0)

    mix = jnp.concatenate([pool_out, sgu_out], axis=-1).astype(jnp.bfloat16)
    mix = _dot(mix, w_out_ref[...])
    h1 = _layer_norm(DEEPNORM_ALPHA * x + mix, ln1_g_ref[...], ln1_b_ref[...])

    gu = _dot(h1.astype(jnp.bfloat16), w_gu_ref[...])
    gate, up = gu[:, :D_FF], gu[:, D_FF:]
    act = (jax.nn.silu(gate) * up).astype(jnp.bfloat16)
    ffn = _dot(act, w_down_ref[...])
    o_ref[...] = _layer_norm(DEEPNORM_ALPHA * h1 + ffn, ln2_g_ref[...], ln2_b_ref[...])


def _resident(shape):
    return pl.BlockSpec(shape, lambda i: (0,) * len(shape), pipeline_mode=pl.Buffered(1))


def kernel(x, w_in, pool_w, pool_scale, sgu_ln_g, sgu_ln_b, sgu_w, sgu_b, w_out, ln1_g, ln1_b,
           w_gate_up, w_down, ln2_g, ln2_b):
    B, S, D = x.shape
    assert D == D_MODEL and S % TM == 0 and TM % CHUNK == 0 and w_in.shape[0] == DEPTH == 1
    n_tok = B * S
    bf = jnp.bfloat16
    row = lambda a: a[0].reshape(1, -1)

    out = pl.pallas_call(
        functools.partial(_layer_kernel, tiles_per_seq=S // TM),
        grid=(n_tok // TM,),
        in_specs=[
            pl.BlockSpec((TM, D), lambda i: (i, 0)),
            _resident((D, IN_COLS)),
            _resident((len(POOL_WINDOWS), POOL_CH, POOL_CH)),
            _resident((1, POOL_WIDTH)),
            _resident((1, SGU_WIDTH)),
            _resident((1, SGU_WIDTH)),
            _resident((SGU_HEADS, CHUNK, CHUNK)),
            _resident((CHUNK, SGU_HEADS)),
            _resident((D, D)),
            _resident((1, D)),
            _resident((1, D)),
            _resident((D, 2 * D_FF)),
            _resident((D_FF, D)),
            _resident((1, D)),
            _resident((1, D)),
        ],
        out_specs=pl.BlockSpec((TM, D), lambda i: (i, 0)),
        out_shape=jax.ShapeDtypeStruct((n_tok, D), x.dtype),
        scratch_shapes=[pltpu.VMEM((HALO, POOL_WIDTH), jnp.float32)],
        compiler_params=pltpu.CompilerParams(
            dimension_semantics=("arbitrary",),
            vmem_limit_bytes=VMEM_LIMIT_BYTES,
        ),
        name="hybrid_layer",
    )(
        x.reshape(n_tok, D),
        w_in[0].astype(bf),
        pool_w[0].astype(bf),
        row(pool_scale), row(sgu_ln_g), row(sgu_ln_b),
        sgu_w[0],
        sgu_b[0].T,
        w_out[0].astype(bf),
        row(ln1_g), row(ln1_b),
        w_gate_up[0].astype(bf),
        w_down[0].astype(bf),
        row(ln2_g), row(ln2_b),
    )
    return out.reshape(B, S, D)
```

```python
import collections
import functools
import math

import jax
import jax.numpy as jnp
from jax import lax
from jax.experimental import pallas as pl
from jax.experimental.pallas import tpu as pltpu

D_MODEL = 1024
POOL_WIDTH = 512
POOL_WINDOWS = (2, 4, 8, 16)
POOL_CH = 128
SGU_WIDTH = 512
SGU_HEADS = 4
SGU_HD = 128
CHUNK = 128
IN_COLS = POOL_WIDTH + 2 * SGU_WIDTH
D_FF = 2816
LN_EPS = 1e-5
DEPTH = 1
DEEPNORM_ALPHA = float((2.0 * DEPTH) ** 0.25)

MXU_COLS = 256
HALO = 16
TM = 512
VMEM_LIMIT_BYTES = 56 * 1024 * 1024

_Weights = collections.namedtuple(
    "_Weights", "w_in pool_w pool_scale sgu_g sgu_b sgu_w sgu_bias w_out ln1_g ln1_b "
                "w_gu w_down ln2_g ln2_b")
_Scratch = collections.namedtuple("_Scratch", "carry resid h1 h1_bf x_bf act")


def _layer_norm(v, g, b):
    mu = jnp.mean(v, axis=-1, keepdims=True)
    d = v - mu
    var = jnp.mean(d * d, axis=-1, keepdims=True)
    return d * lax.rsqrt(var + LN_EPS) * g + b


def _gelu_exact(z):
    return 0.5 * z * (1.0 + lax.erf(z * math.sqrt(0.5)))


def _dot(a, b):
    return jnp.dot(a, b, preferred_element_type=jnp.float32)


def _cols(k):
    return slice(k * MXU_COLS, (k + 1) * MXU_COLS)


def _pool(xp, hist, pos0, w):
    ext = jnp.concatenate([hist, xp], axis=0)
    pos = pos0 + lax.broadcasted_iota(jnp.int32, (TM, POOL_CH), 0) + 1
    out = []
    for g, win in enumerate(POOL_WINDOWS):
        s = ext[:, g * POOL_CH:(g + 1) * POOL_CH]
        shift = 1
        while shift < win:
            s = s + pltpu.roll(s, shift, 0)
            shift *= 2
        cnt = jnp.minimum(pos, win).astype(jnp.float32)
        pooled = s[HALO:, :] / cnt - xp[:, g * POOL_CH:(g + 1) * POOL_CH]
        out.append(_dot(pooled.astype(jnp.bfloat16), w.pool_w[g]))
    return jnp.concatenate(out, axis=-1) * w.pool_scale[...]


def _spatial_gate(v, w):
    tril = (lax.broadcasted_iota(jnp.int32, (CHUNK, CHUNK), 0)
            >= lax.broadcasted_iota(jnp.int32, (CHUNK, CHUNK), 1))
    n_chunks = TM // CHUNK
    per_head = []
    for h in range(SGU_HEADS):
        ws = jnp.where(tril, w.sgu_w[h], 0.0).astype(jnp.bfloat16)
        bias = jnp.broadcast_to(w.sgu_bias[:, h:h + 1], (CHUNK, SGU_HD))
        vh = jnp.concatenate(
            [v[c * CHUNK:(c + 1) * CHUNK, h * SGU_HD:(h + 1) * SGU_HD] for c in range(n_chunks)],
            axis=1)
        mixed = _dot(ws, vh)
        per_head.append([mixed[:, c * SGU_HD:(c + 1) * SGU_HD] + bias for c in range(n_chunks)])
    return jnp.concatenate(
        [jnp.concatenate([per_head[h][c] for h in range(SGU_HEADS)], axis=1)
         for c in range(n_chunks)], axis=0)


def _layer_kernel(x_ref, w_in, pool_w, pool_scale, sgu_g, sgu_b, sgu_w, sgu_bias, w_out,
                  ln1_g, ln1_b, w_gu, w_down, ln2_g, ln2_b, o_ref,
                  carry_ref, resid_ref, h1_ref, h1_bf_ref, x_bf_ref, act_ref, *, tiles_per_seq):
    w = _Weights(w_in, pool_w, pool_scale, sgu_g, sgu_b, sgu_w, sgu_bias, w_out, ln1_g, ln1_b,
                 w_gu, w_down, ln2_g, ln2_b)
    step = pl.program_id(0)
    tile_in_seq = step % tiles_per_seq

    @pl.when(step == 0)
    def _():
        resid_ref[...] = jnp.zeros_like(resid_ref)

    @pl.when(tile_in_seq == 0)
    def _():
        carry_ref[...] = jnp.zeros_like(carry_ref)

    h1 = _layer_norm(resid_ref[...], w.ln1_g[...], w.ln1_b[...])
    h1_ref[...] = h1
    h1_bf_ref[...] = h1.astype(jnp.bfloat16)

    def gate_up(k):
        gate = _dot(h1_bf_ref[...], w.w_gu[:, _cols(k)])
        up = _dot(h1_bf_ref[...], w.w_gu[:, pl.ds(D_FF + k * MXU_COLS, MXU_COLS)])
        act_ref[:, _cols(k)] = (jax.nn.silu(gate) * up).astype(jnp.bfloat16)

    x_bf_ref[...] = x_ref[...].astype(jnp.bfloat16)

    def proj(j):
        return _dot(x_bf_ref[...], w.w_in[:, _cols(j)])

    n_ff = D_FF // MXU_COLS
    xp = jnp.concatenate([proj(0), proj(1)], axis=-1)
    gate_up(0)
    u0 = _gelu_exact(proj(2))
    gate_up(1)
    u1 = _gelu_exact(proj(3))
    gate_up(2)
    v0 = _gelu_exact(proj(4))
    gate_up(3)
    v1 = _gelu_exact(proj(5))
    gate_up(4)
    pool_out = _pool(xp, carry_ref[...], tile_in_seq * TM, w)
    carry_ref[...] = xp[TM - HALO:, :]
    gate_up(5)
    v = _layer_norm(jnp.concatenate([v0, v1], axis=-1), w.sgu_g[...], w.sgu_b[...])
    gate_up(6)
    sgu_out = jnp.concatenate([u0, u1], axis=-1) * _spatial_gate(v.astype(jnp.bfloat16), w)
    gate_up(7)
    mix_in = jnp.concatenate([pool_out, sgu_out], axis=-1).astype(jnp.bfloat16)
    for k in range(8, n_ff):
        gate_up(k)

    ffn = _dot(act_ref[...], w.w_down[...])
    mix = _dot(mix_in, w.w_out[...])
    o_ref[...] = _layer_norm(DEEPNORM_ALPHA * h1_ref[...] + ffn, w.ln2_g[...], w.ln2_b[...])
    resid_ref[...] = DEEPNORM_ALPHA * x_ref[...] + mix


def _resident(shape):
    return pl.BlockSpec(shape, lambda i: (0,) * len(shape), pipeline_mode=pl.Buffered(1))


def kernel(x, w_in, pool_w, pool_scale, sgu_ln_g, sgu_ln_b, sgu_w, sgu_b, w_out, ln1_g, ln1_b,
           w_gate_up, w_down, ln2_g, ln2_b):
    B, S, D = x.shape
    assert D == D_MODEL and S % TM == 0 and TM % CHUNK == 0 and w_in.shape[0] == DEPTH == 1
    n_tok = B * S
    n_tiles = n_tok // TM
    bf = jnp.bfloat16
    row = lambda a: a[0].reshape(1, -1)

    out = pl.pallas_call(
        functools.partial(_layer_kernel, tiles_per_seq=S // TM),
        grid=(n_tiles + 1,),
        in_specs=[
            pl.BlockSpec((TM, D), lambda i: (jnp.minimum(i, n_tiles - 1), 0)),
            _resident((D, IN_COLS)),
            _resident((len(POOL_WINDOWS), POOL_CH, POOL_CH)),
            _resident((1, POOL_WIDTH)),
            _resident((1, SGU_WIDTH)),
            _resident((1, SGU_WIDTH)),
            _resident((SGU_HEADS, CHUNK, CHUNK)),
            _resident((CHUNK, SGU_HEADS)),
            _resident((D, D)),
            _resident((1, D)),
            _resident((1, D)),
            _resident((D, 2 * D_FF)),
            _resident((D_FF, D)),
            _resident((1, D)),
            _resident((1, D)),
        ],
        out_specs=pl.BlockSpec((TM, D), lambda i: (jnp.maximum(i - 1, 0), 0)),
        out_shape=jax.ShapeDtypeStruct((n_tok, D), x.dtype),
        scratch_shapes=[
            pltpu.VMEM((HALO, POOL_WIDTH), jnp.float32),
            pltpu.VMEM((TM, D), jnp.float32),
            pltpu.VMEM((TM, D), jnp.float32),
            pltpu.VMEM((TM, D), bf),
            pltpu.VMEM((TM, D), bf),
            pltpu.VMEM((TM, D_FF), bf),
        ],
        compiler_params=pltpu.CompilerParams(
            dimension_semantics=("arbitrary",),
            vmem_limit_bytes=VMEM_LIMIT_BYTES,
        ),
        name="hybrid_layer",
    )(
        x.reshape(n_tok, D),
        w_in[0].astype(bf),
        pool_w[0].astype(bf),
        row(pool_scale), row(sgu_ln_g), row(sgu_ln_b),
        sgu_w[0],
        sgu_b[0].T,
        w_out[0].astype(bf),
        row(ln1_g), row(ln1_b),
        w_gate_up[0].astype(bf),
        w_down[0].astype(bf),
        row(ln2_g), row(ln2_b),
    )
    return out.reshape(B, S, D)
```
